```python
import math
import jax
import jax.numpy as jnp
from jax import lax
import numpy as np

D_MODEL = 1024
BATCH = 4
SEQ = 8192
DEPTH = 4

GRID_W = 64
CTX_LEN = 256
RMS_EPS = 1e-6
N_MOD = 6

SSD_HEADS = 8
SSD_HEAD_DIM = 64
SSD_INNER = SSD_HEADS * SSD_HEAD_DIM
SSD_GROUPS = 2
HEADS_PER_GROUP = SSD_HEADS // SSD_GROUPS
SSD_STATE = 128
SSD_CONV = 3
SSD_CHUNK = 128
SSD_CONV_DIM = SSD_INNER + 2 * SSD_GROUPS * SSD_STATE

FOURIER_GROUPS = 4
FOURIER_GROUP_DIM = 128
FOURIER_WIDTH = FOURIER_GROUPS * FOURIER_GROUP_DIM

EVEN_IN = SSD_INNER + SSD_CONV_DIM + 2 * SSD_HEADS + FOURIER_WIDTH
EVEN_MIX = SSD_INNER + FOURIER_WIDTH

NA_HEADS = 16
NA_HEAD_DIM = D_MODEL // NA_HEADS
NA_WIN_ROWS = 8
NA_WIN_COLS = 16

N_EXPERTS = 16
EXPERT_FF = 2048
EC_CAPACITY_FACTOR = 2

kernel_name = 'hybrid_ssd_fnet_natten_ecmoe_dit'


def rms_norm(x, g):
    x32 = x.astype(jnp.float32)
    y = x32 * lax.rsqrt(jnp.mean(x32 * x32, axis=-1, keepdims=True) + RMS_EPS)
    return (y * g.astype(jnp.float32)).astype(x.dtype)


def centred_dwconv(u, w, b):
    k_w = w.shape[0]
    pad = k_w // 2
    n = u.shape[1]
    up = jnp.pad(u, ((0, 0), (pad, pad), (0, 0)))
    out = b
    for k in range(k_w):
        out = out + up[:, k:k + n] * w[k]
    return out


def ssd_chunked(x, dt, a, bm, cm, h0):
    bsz, n = x.shape[:2]
    q = SSD_CHUNK
    nc = n // q
    xc = x.reshape(bsz, nc, q, SSD_HEADS, SSD_HEAD_DIM)
    dtc = dt.reshape(bsz, nc, q, SSD_HEADS)
    bc = bm.reshape(bsz, nc, q, SSD_GROUPS, SSD_STATE)
    cc = cm.reshape(bsz, nc, q, SSD_GROUPS, SSD_STATE)
    cum = jnp.cumsum(dtc * a, axis=2)
    cum_h = jnp.moveaxis(cum, -1, 2)
    seg = cum_h[..., :, None] - cum_h[..., None, :]
    ordered = jnp.tril(jnp.ones((q, q), dtype=bool))
    decay = jnp.exp(jnp.where(ordered, seg, -jnp.inf))
    cb = jnp.repeat(jnp.einsum('bcign,bcjgn->bcgij', cc, bc), HEADS_PER_GROUP, axis=2)
    xdt = xc * dtc[..., None]
    y_diag = jnp.einsum('bchij,bcjhp->bcihp', cb * decay, xdt)
    to_end = jnp.exp(cum[:, :, -1:, :] - cum)
    bh = jnp.repeat(bc, HEADS_PER_GROUP, axis=3)
    states = jnp.einsum('bcjhn,bcjhp->bchpn', bh * to_end[..., None], xdt)
    chunk_decay = jnp.exp(cum[:, :, -1, :])

    def step(h, inp):
        dec, st = inp
        return dec[:, :, None, None] * h + st, h

    h_final, h_start = lax.scan(step, h0, (jnp.moveaxis(chunk_decay, 1, 0), jnp.moveaxis(states, 1, 0)))
    h_start = jnp.moveaxis(h_start, 0, 1)
    ch = jnp.repeat(cc, HEADS_PER_GROUP, axis=3)
    y_off = jnp.einsum('bcihn,bchpn->bcihp', ch, h_start) * jnp.exp(cum)[..., None]
    return (y_diag + y_off).reshape(bsz, n, SSD_HEADS, SSD_HEAD_DIM), h_final


def ssd_bidirectional(xbc, dt_raw, a_log, dt_bias, d_skip, h0_fwd, h0_bwd):
    bsz, n = xbc.shape[:2]
    xbc = xbc.astype(jnp.float32)
    gn = SSD_GROUPS * SSD_STATE
    x = xbc[..., :SSD_INNER].reshape(bsz, n, SSD_HEADS, SSD_HEAD_DIM)
    bm = xbc[..., SSD_INNER:SSD_INNER + gn].reshape(bsz, n, SSD_GROUPS, SSD_STATE)
    cm = xbc[..., SSD_INNER + gn:].reshape(bsz, n, SSD_GROUPS, SSD_STATE)
    dt = jax.nn.softplus(dt_raw.astype(jnp.float32) + dt_bias.astype(jnp.float32))
    a = -jnp.exp(a_log.astype(jnp.float32))
    y_f, h_f = ssd_chunked(x, dt[:, :, 0], a[0], bm, cm, h0_fwd)
    flip = lambda t: jnp.flip(t, axis=1)
    y_b, h_b = ssd_chunked(flip(x), flip(dt[:, :, 1]), a[1], flip(bm), flip(cm), h0_bwd)
    y = y_f + flip(y_b) + d_skip.astype(jnp.float32)[:, None] * x
    return y, h_f, h_b


def fourier_mix(u):
    bsz, n = u.shape[:2]
    ug = u.astype(jnp.float32).reshape(bsz, n, FOURIER_GROUPS, FOURIER_GROUP_DIM)
    f = jnp.fft.fft2(ug, axes=(1, 3), norm='ortho').real
    return f.reshape(bsz, n, FOURIER_WIDTH)


def even_mixer(hl, hc, w_in, conv_w, conv_b, a_log, dt_bias, d_skip, g_ssd, w_out, need_ctx):
    o_dt = SSD_INNER + SSD_CONV_DIM

    def project(h):
        pr = h @ w_in
        z = pr[..., :SSD_INNER]
        xbc = jax.nn.silu(centred_dwconv(pr[..., SSD_INNER:o_dt], conv_w, conv_b))
        dt_raw = pr[..., o_dt:o_dt + 2 * SSD_HEADS].reshape(h.shape[0], h.shape[1], 2, SSD_HEADS)
        u = pr[..., o_dt + 2 * SSD_HEADS:]
        return z, xbc, dt_raw, u

    def finish(z, y, u):
        bsz, n = y.shape[:2]
        yg = y.reshape(bsz, n, SSD_INNER) * jax.nn.silu(z.astype(jnp.float32))
        yg = yg.reshape(bsz, n, SSD_GROUPS, SSD_INNER // SSD_GROUPS)
        yg = yg * lax.rsqrt(jnp.mean(yg * yg, axis=-1, keepdims=True) + RMS_EPS)
        yg = yg.reshape(bsz, n, SSD_INNER) * g_ssd.astype(jnp.float32)
        mix = jnp.concatenate([yg, fourier_mix(u)], axis=-1).astype(z.dtype)
        return mix @ w_out

    bsz = hl.shape[0]
    zeros = jnp.zeros((bsz, SSD_HEADS, SSD_HEAD_DIM, SSD_STATE), jnp.float32)
    zc, xbcc, dtc, uc = project(hc)
    yc, hc_f, hc_b = ssd_bidirectional(xbcc, dtc, a_log, dt_bias, d_skip, zeros, zeros)
    zl, xbcl, dtl, ul = project(hl)
    yl, _, _ = ssd_bidirectional(xbcl, dtl, a_log, dt_bias, d_skip, hc_f, hc_b)
    out_l = finish(zl, yl, ul)
    out_c = finish(zc, yc, uc) if need_ctx else None
    return out_l, out_c


def odd_mixer(hl, hc, w_qkv, rpb, w_o, need_ctx):
    bsz, n, _ = hl.shape
    lc = hc.shape[1]
    rows = n // GRID_W
    kr = min(NA_WIN_ROWS, rows)
    kc = NA_WIN_COLS
    scale = NA_HEAD_DIM ** -0.5
    qkv_l = (hl @ w_qkv).reshape(bsz, rows, GRID_W, 3, NA_HEADS, NA_HEAD_DIM)
    ql = qkv_l[:, :, :, 0] * scale
    kl = qkv_l[:, :, :, 1]
    vl = qkv_l[:, :, :, 2]
    qkv_c = (hc @ w_qkv).reshape(bsz, lc, 3, NA_HEADS, NA_HEAD_DIM)
    kctx = qkv_c[:, :, 1]
    vctx = qkv_c[:, :, 2]
    cols = jnp.arange(GRID_W)
    col_start = jnp.clip(cols - kc // 2, 0, GRID_W - kc)
    col_idx = col_start[:, None] + jnp.arange(kc)
    col_off = col_idx - cols[:, None] + NA_WIN_COLS - 1

    def row_block(r):
        rs = jnp.clip(r - kr // 2, 0, rows - kr)
        q = lax.dynamic_index_in_dim(ql, r, axis=1, keepdims=False)
        k_rows = lax.dynamic_slice_in_dim(kl, rs, kr, axis=1)
        v_rows = lax.dynamic_slice_in_dim(vl, rs, kr, axis=1)
        k_nb = k_rows[:, :, col_idx]
        v_nb = v_rows[:, :, col_idx]
        row_off = rs + jnp.arange(kr) - r + NA_WIN_ROWS - 1
        bias = rpb[:, row_off[:, None, None], col_off[None, :, :]]
        s_lat = jnp.einsum('bqhd,biqjhd->bhqij', q, k_nb).astype(jnp.float32)
        s_lat = s_lat + jnp.transpose(bias, (0, 2, 1, 3)).astype(jnp.float32)[None]
        s_lat = s_lat.reshape(bsz, NA_HEADS, GRID_W, kr * kc)
        s_ctx = jnp.einsum('bqhd,bkhd->bhqk', q, kctx).astype(jnp.float32)
        p = jax.nn.softmax(jnp.concatenate([s_lat, s_ctx], axis=-1), axis=-1).astype(vl.dtype)
        p_lat = p[..., :kr * kc].reshape(bsz, NA_HEADS, GRID_W, kr, kc)
        p_ctx = p[..., kr * kc:]
        return (jnp.einsum('bhqij,biqjhd->bqhd', p_lat, v_nb)
                + jnp.einsum('bhqk,bkhd->bqhd', p_ctx, vctx))

    o_rows = lax.map(row_block, jnp.arange(rows))
    out_l = jnp.moveaxis(o_rows, 0, 1).reshape(bsz, n, D_MODEL) @ w_o
    out_c = None
    if need_ctx:
        qc = qkv_c[:, :, 0] * scale
        s = jnp.einsum('bqhd,bkhd->bhqk', qc, kctx).astype(jnp.float32)
        p = jax.nn.softmax(s, axis=-1).astype(vctx.dtype)
        out_c = jnp.einsum('bhqk,bkhd->bqhd', p, vctx).reshape(bsz, lc, D_MODEL) @ w_o
    return out_l, out_c


def ec_moe(h, w_router, w1, w3, w2):
    bsz, n, d = h.shape
    cap = EC_CAPACITY_FACTOR * n // N_EXPERTS
    aff = jax.nn.softmax((h @ w_router).astype(jnp.float32), axis=-1)
    gate, idx = lax.top_k(jnp.swapaxes(aff, 1, 2), cap)
    xg = jax.vmap(lambda hb, ib: hb[ib])(h, idx)
    a = jnp.einsum('becd,edf->becf', xg, w1)
    b = jnp.einsum('becd,edf->becf', xg, w3)
    y = jnp.einsum('becf,efd->becd', jax.nn.silu(a) * b, w2) * gate[..., None].astype(h.dtype)
    return jax.vmap(lambda yb, ib: jnp.zeros((n, d), yb.dtype).at[ib].add(yb))(y, idx)


def setup_inputs(seed: int = 0) -> dict:
    key = jax.random.key(seed)
    ks = jax.random.split(key, 24)
    f32 = jnp.float32
    ne = (DEPTH + 1) // 2
    no = DEPTH // 2
    nrm = lambda k, shape, s: jax.random.normal(k, shape, f32) * s
    dt0 = jnp.exp(jax.random.uniform(ks[12], (ne, 2, SSD_HEADS), f32, math.log(1e-3), math.log(1e-1)))
    return {
        'x': nrm(ks[0], (BATCH, SEQ, D_MODEL), 1.0),
        'c': nrm(ks[1], (BATCH, D_MODEL), 1.0),
        'ctx': nrm(ks[2], (BATCH, CTX_LEN, D_MODEL), 1.0),
        'c_ctx': nrm(ks[3], (D_MODEL,), 1.0),
        'w_mod': nrm(ks[4], (DEPTH, D_MODEL, N_MOD * D_MODEL), 0.5 * D_MODEL ** -0.5),
        'b_mod': nrm(ks[5], (DEPTH, N_MOD * D_MODEL), 0.02),
        'g_mix': 1.0 + nrm(ks[6], (DEPTH, D_MODEL), 0.02),
        'g_ffn': 1.0 + nrm(ks[7], (DEPTH, D_MODEL), 0.02),
        'w_in_e': nrm(ks[8], (ne, D_MODEL, EVEN_IN), D_MODEL ** -0.5),
        'conv_w': nrm(ks[9], (ne, SSD_CONV, SSD_CONV_DIM), SSD_CONV ** -0.5),
        'conv_b': nrm(ks[10], (ne, SSD_CONV_DIM), 0.02),
        'a_log': jnp.log(jax.random.uniform(ks[11], (ne, 2, SSD_HEADS), f32, 1.0, 16.0)),
        'dt_bias': dt0 + jnp.log(-jnp.expm1(-dt0)),
        'd_skip': 1.0 + nrm(ks[13], (ne, SSD_HEADS), 0.1),
        'g_ssd': 1.0 + nrm(ks[14], (ne, SSD_INNER), 0.02),
        'w_out_e': nrm(ks[15], (ne, EVEN_MIX, D_MODEL), EVEN_MIX ** -0.5),
        'w_qkv': nrm(ks[16], (no, D_MODEL, 3 * D_MODEL), D_MODEL ** -0.5),
        'rpb': nrm(ks[17], (no, NA_HEADS, 2 * NA_WIN_ROWS - 1, 2 * NA_WIN_COLS - 1), 0.1),
        'w_o': nrm(ks[18], (no, D_MODEL, D_MODEL), D_MODEL ** -0.5),
        'w_router': nrm(ks[19], (DEPTH, D_MODEL, N_EXPERTS), D_MODEL ** -0.5),
        'w_e1': nrm(ks[20], (DEPTH, N_EXPERTS, D_MODEL, EXPERT_FF), D_MODEL ** -0.5),
        'w_e3': nrm(ks[21], (DEPTH, N_EXPERTS, D_MODEL, EXPERT_FF), D_MODEL ** -0.5),
        'w_e2': nrm(ks[22], (DEPTH, N_EXPERTS, EXPERT_FF, D_MODEL), EXPERT_FF ** -0.5),
        'g_final': 1.0 + nrm(ks[23], (D_MODEL,), 0.02),
    }


def reference(x, c, ctx, c_ctx, w_mod, b_mod, g_mix, g_ffn, w_in_e, conv_w, conv_b, a_log, dt_bias,
              d_skip, g_ssd, w_out_e, w_qkv, rpb, w_o, w_router, w_e1, w_e3, w_e2, g_final):
    xl = x
    xc = ctx
    silu_c = jax.nn.silu(c)
    silu_cc = jax.nn.silu(c_ctx)
    for i in range(DEPTH):
        need_ctx = i < DEPTH - 1
        j = i // 2
        mod_l = (silu_c @ w_mod[i] + b_mod[i])[:, None, :]
        mod_c = (silu_cc @ w_mod[i] + b_mod[i])[None, None, :]
        sh1_l, sc1_l, ga1_l, sh2_l, sc2_l, ga2_l = jnp.split(mod_l, N_MOD, axis=-1)
        sh1_c, sc1_c, ga1_c, sh2_c, sc2_c, ga2_c = jnp.split(mod_c, N_MOD, axis=-1)
        hl = rms_norm(xl, g_mix[i]) * (1.0 + sc1_l) + sh1_l
        hc = rms_norm(xc, g_mix[i]) * (1.0 + sc1_c) + sh1_c
        if i % 2 == 0:
            out_l, out_c = even_mixer(hl, hc, w_in_e[j], conv_w[j], conv_b[j], a_log[j], dt_bias[j],
                                      d_skip[j], g_ssd[j], w_out_e[j], need_ctx)
        else:
            out_l, out_c = odd_mixer(hl, hc, w_qkv[j], rpb[j], w_o[j], need_ctx)
        xl = xl + ga1_l * out_l
        hl2 = rms_norm(xl, g_ffn[i]) * (1.0 + sc2_l) + sh2_l
        xl = xl + ga2_l * ec_moe(hl2, w_router[i], w_e1[i], w_e3[i], w_e2[i])
        if need_ctx:
            xc = xc + ga1_c * out_c
            hc2 = rms_norm(xc, g_ffn[i]) * (1.0 + sc2_c) + sh2_c
            xc = xc + ga2_c * ec_moe(hc2, w_router[i], w_e1[i], w_e3[i], w_e2[i])
    return rms_norm(xl, g_final)
```

```python
import functools
import math

import numpy as np
import jax
import jax.numpy as jnp
from jax import lax
from jax.experimental import pallas as pl
from jax.experimental.pallas import tpu as pltpu

F32 = jnp.float32
BF16 = jnp.bfloat16

D_MODEL = 1024
BATCH = 4
SEQ = 8192
DEPTH = 4
GRID_W = 64
GRID_H = SEQ // GRID_W
CTX_LEN = 256
TOK = SEQ + CTX_LEN
RMS_EPS = 1e-6
N_MOD = 6

SSD_HEADS = 8
SSD_HEAD_DIM = 64
SSD_INNER = 512
SSD_GROUPS = 2
SSD_STATE = 128
SSD_CHUNK = 128
SSD_CONV_DIM = 1024
GROUP_W = SSD_INNER // SSD_GROUPS
FOURIER_GROUPS = 4
FOURIER_GROUP_DIM = 128
FOURIER_WIDTH = 512
DT_PAD = 128
EVEN_IN_PAD = SSD_CONV_DIM + SSD_INNER + FOURIER_WIDTH + DT_PAD

NA_HEADS = 16
NA_HEAD_DIM = 64
NA_WIN_ROWS = 8
NA_WIN_COLS = 16
NA_QROWS = 4
NA_KROWS = NA_QROWS + NA_WIN_ROWS - 1
NA_Q = NA_QROWS * GRID_W
NA_K = NA_KROWS * GRID_W
NEG_BIG = -1e30

N_EXPERTS = 16
EXPERT_FF = 2048
CAP_LAT = 2 * SEQ // N_EXPERTS
CAP_CTX = 2 * CTX_LEN // N_EXPERTS
CAP_ALL = CAP_LAT + CAP_CTX
FF_TILE = 512

ROW_TILE = 256
N_LAT_TILES = SEQ // ROW_TILE
V7X_VMEM_BYTES = 64 * 1024 * 1024


def _params(sem, vmem_mb=40):
    assert vmem_mb * 1024 * 1024 < V7X_VMEM_BYTES
    return pltpu.CompilerParams(dimension_semantics=sem, vmem_limit_bytes=vmem_mb * 1024 * 1024)


def _mod_index(b, t):
    return (jnp.where(t < N_LAT_TILES, b, BATCH + b), 0, 0)


def _silu(v):
    return v * jax.nn.sigmoid(v)


def _split3(v):
    hi = v.astype(BF16)
    r1 = v - hi.astype(F32)
    mid = r1.astype(BF16)
    lo = (r1 - mid.astype(F32)).astype(BF16)
    return hi, mid, lo


def _sel_left(mat, v):
    n = v.shape[1]
    cat = jnp.concatenate(_split3(v), axis=1)
    r = jnp.dot(mat, cat, preferred_element_type=F32)
    return r[:, :n] + r[:, n:2 * n] + r[:, 2 * n:]


def _sel_right(v, mat):
    m = v.shape[0]
    cat = jnp.concatenate(_split3(v), axis=0)
    r = jnp.dot(cat, mat, preferred_element_type=F32)
    return r[:m] + r[m:2 * m] + r[2 * m:]


def _mod_kernel(c_ref, w_ref, b_ref, o_ref):
    s = _silu(c_ref[...])
    hi, mid, lo = _split3(s)
    w = w_ref[0]
    w_hi = w.astype(BF16)
    w_lo = (w - w_hi.astype(F32)).astype(BF16)
    acc = jnp.dot(hi, w_hi, preferred_element_type=F32)
    acc += jnp.dot(hi, w_lo, preferred_element_type=F32)
    acc += jnp.dot(mid, w_hi, preferred_element_type=F32)
    acc += jnp.dot(lo, w_hi, preferred_element_type=F32)
    acc += jnp.dot(mid, w_lo, preferred_element_type=F32)
    o_ref[0] = acc + b_ref[0]


def _modulation(cc, w_mod, b_mod):
    tn = 1536
    return pl.pallas_call(
        _mod_kernel,
        grid=(DEPTH, N_MOD * D_MODEL // tn),
        in_specs=[pl.BlockSpec((8, D_MODEL), lambda i, j: (0, 0)),
                  pl.BlockSpec((1, D_MODEL, tn), lambda i, j: (i, 0, j)),
                  pl.BlockSpec((1, 1, tn), lambda i, j: (i, 0, j))],
        out_specs=pl.BlockSpec((1, 8, tn), lambda i, j: (i, 0, j)),
        out_shape=jax.ShapeDtypeStruct((DEPTH, 8, N_MOD * D_MODEL), F32),
        compiler_params=_params(("arbitrary", "arbitrary")),
    )(cc, w_mod, b_mod.reshape(DEPTH, 1, N_MOD * D_MODEL))


def _norm_mod(x, g, sc, sh):
    y = x * lax.rsqrt(jnp.mean(x * x, axis=-1, keepdims=True) + RMS_EPS) * g
    return y * (1.0 + sc) + sh


def _proj_kernel(x_ref, g_ref, sc_ref, sh_ref, w_ref, *o_refs, widths):
    h = _norm_mod(x_ref[0], g_ref[...], sc_ref[0], sh_ref[0]).astype(BF16)
    r = jnp.dot(h, w_ref[...], preferred_element_type=F32)
    off = 0
    for o_ref, wd in zip(o_refs, widths):
        o_ref[0] = r[:, off:off + wd].astype(o_ref.dtype)
        off += wd


def _project(x, g, sc, sh, w, widths, dtype):
    n = w.shape[1]
    assert sum(widths) == n
    return pl.pallas_call(
        functools.partial(_proj_kernel, widths=widths),
        grid=(BATCH, TOK // ROW_TILE),
        in_specs=[pl.BlockSpec((1, ROW_TILE, D_MODEL), lambda b, t: (b, t, 0)),
                  pl.BlockSpec((1, D_MODEL), lambda b, t: (0, 0)),
                  pl.BlockSpec((1, 1, D_MODEL), _mod_index),
                  pl.BlockSpec((1, 1, D_MODEL), _mod_index),
                  pl.BlockSpec((D_MODEL, n), lambda b, t: (0, 0))],
        out_specs=[pl.BlockSpec((1, ROW_TILE, wd), lambda b, t: (b, t, 0)) for wd in widths],
        out_shape=[jax.ShapeDtypeStruct((BATCH, TOK, wd), dtype) for wd in widths],
        compiler_params=_params(("parallel", "arbitrary"), 48),
    )(x, g.reshape(1, D_MODEL), sc, sh, w)


def _conv_kernel(x_ref, prev_ref, next_ref, w_ref, b_ref, o_ref):
    t = pl.program_id(1)
    last = pl.num_programs(1) - 1
    x = x_ref[0]
    seg_start = jnp.logical_or(t == 0, t == last)
    seg_end = jnp.logical_or(t == last - 1, t == last)
    prev_row = jnp.where(seg_start, 0.0, prev_ref[0, 7:8, :])
    next_row = jnp.where(seg_end, 0.0, next_ref[0, 0:1, :])
    rows = lax.broadcasted_iota(jnp.int32, x.shape, 0)
    x_prev = jnp.where(rows == 0, prev_row, pltpu.roll(x, 1, axis=0))
    x_next = jnp.where(rows == x.shape[0] - 1, next_row, pltpu.roll(x, x.shape[0] - 1, axis=0))
    w = w_ref[...]
    acc = b_ref[...] + x_prev * w[0:1] + x * w[1:2] + x_next * w[2:3]
    o_ref[0] = _silu(acc)


def _conv_silu(xbc_pre, conv_w, conv_b):
    rb = ROW_TILE // 8
    nrb = TOK // 8
    return pl.pallas_call(
        _conv_kernel,
        grid=(BATCH, TOK // ROW_TILE),
        in_specs=[pl.BlockSpec((1, ROW_TILE, SSD_CONV_DIM), lambda b, t: (b, t, 0)),
                  pl.BlockSpec((1, 8, SSD_CONV_DIM), lambda b, t: (b, jnp.maximum(t * rb - 1, 0), 0)),
                  pl.BlockSpec((1, 8, SSD_CONV_DIM), lambda b, t: (b, jnp.minimum((t + 1) * rb, nrb - 1), 0)),
                  pl.BlockSpec((8, SSD_CONV_DIM), lambda b, t: (0, 0)),
                  pl.BlockSpec((1, SSD_CONV_DIM), lambda b, t: (0, 0))],
        out_specs=pl.BlockSpec((1, ROW_TILE, SSD_CONV_DIM), lambda b, t: (b, t, 0)),
        out_shape=jax.ShapeDtypeStruct((BATCH, TOK, SSD_CONV_DIM), F32),
        compiler_params=_params(("parallel", "arbitrary")),
    )(xbc_pre, xbc_pre, xbc_pre, jnp.pad(conv_w, ((0, 5), (0, 0))), conv_b.reshape(1, SSD_CONV_DIM))


def _ssd_constants(direction):
    q = SSD_CHUNK
    i = np.arange(q)
    tri = (i[None, :] <= i[:, None]) if direction == 0 else (i[None, :] >= i[:, None])
    lanes = np.arange(DT_PAD)
    e_x = np.zeros((DT_PAD, SSD_INNER), np.float32)
    e_col = np.zeros((DT_PAD, SSD_HEADS * q), np.float32)
    for h in range(SSD_HEADS):
        e_x[direction * SSD_HEADS + h, h * SSD_HEAD_DIM:(h + 1) * SSD_HEAD_DIM] = 1.0
        e_col[direction * SSD_HEADS + h, h * q:(h + 1) * q] = 1.0
    del lanes
    return (jnp.asarray(tri, BF16), jnp.asarray(e_x, BF16), jnp.asarray(e_col, BF16))


def _ssd_kernel(*refs, direction, finish):
    if finish:
        (xbc_ref, dt_ref, z_ref, yf_ref, h0_ref, dtb_ref, a_ref, tri_ref, ex_ref, ecol_ref, dskip_ref, gssd_ref,
         out_ref, hout_ref, h_sc) = refs
    else:
        (xbc_ref, dt_ref, h0_ref, dtb_ref, a_ref, tri_ref, ex_ref, ecol_ref,
         out_ref, hout_ref, h_sc) = refs
    q = SSD_CHUNK
    c = pl.program_id(1)

    @pl.when(c == 0)
    def _():
        h_sc[...] = h0_ref[0]

    xbc = xbc_ref[0]
    x = xbc[:, :SSD_INNER]
    v = dt_ref[0] + dtb_ref[...]
    dt = jnp.maximum(v, 0.0) + jnp.log1p(jnp.exp(-jnp.abs(v)))
    dta = dt * a_ref[...]
    cum = _sel_left(tri_ref[...], dta)
    cum_x = _sel_right(cum, ex_ref[...])
    dt_x = _sel_right(dt, ex_ref[...])
    cum_col = _sel_right(cum, ecol_ref[...])
    cum_row = cum.T
    end_row = q - 1 if direction == 0 else 0
    cum_end = cum_x[end_row:end_row + 1, :]
    xdt = x * dt_x
    x_end = (xdt * jnp.exp(cum_end - cum_x)).astype(BF16)
    grow = jnp.exp(cum_x)
    chunk_decay = jnp.exp(cum_end)
    ri = lax.broadcasted_iota(jnp.int32, (q, q), 0)
    ci = lax.broadcasted_iota(jnp.int32, (q, q), 1)
    ordered = (ci <= ri) if direction == 0 else (ci >= ri)
    lane_g = lax.broadcasted_iota(jnp.int32, (q, GROUP_W), 1) // SSD_HEAD_DIM
    ys = []
    for g in range(SSD_GROUPS):
        bg = xbc[:, SSD_INNER + g * SSD_STATE:SSD_INNER + (g + 1) * SSD_STATE].astype(BF16)
        cg = xbc[:, SSD_INNER + (SSD_GROUPS + g) * SSD_STATE:SSD_INNER + (SSD_GROUPS + g + 1) * SSD_STATE].astype(BF16)
        cb = lax.dot_general(cg, bg, (((1,), (1,)), ((), ())), preferred_element_type=F32)
        xg = xdt[:, g * GROUP_W:(g + 1) * GROUP_W]
        ms, rhs = [], []
        for hh in range(SSD_HEADS // SSD_GROUPS):
            h = g * (SSD_HEADS // SSD_GROUPS) + hh
            lane = direction * SSD_HEADS + h
            seg = cum_col[:, h * q:(h + 1) * q] - cum_row[lane:lane + 1, :]
            dec = jnp.exp(jnp.where(ordered, seg, NEG_BIG))
            ms.append((cb * dec).astype(BF16))
            rhs.append(jnp.where(lane_g == hh, xg, 0.0).astype(BF16))
        y_diag = jnp.dot(jnp.concatenate(ms, axis=1), jnp.concatenate(rhs, axis=0), preferred_element_type=F32)
        h_start = h_sc[g]
        y_off = jnp.dot(cg, h_start.astype(BF16), preferred_element_type=F32) * grow[:, g * GROUP_W:(g + 1) * GROUP_W]
        st = lax.dot_general(bg, x_end[:, g * GROUP_W:(g + 1) * GROUP_W], (((0,), (0,)), ((), ())),
                             preferred_element_type=F32)
        h_sc[g] = chunk_decay[:, g * GROUP_W:(g + 1) * GROUP_W] * h_start + st
        ys.append(y_diag + y_off)
    y = jnp.concatenate(ys, axis=1)

    if finish:
        y = y + yf_ref[0] + dskip_ref[...] * x
        yg = y * _silu(z_ref[0])
        parts = []
        for g in range(SSD_GROUPS):
            p = yg[:, g * GROUP_W:(g + 1) * GROUP_W]
            parts.append(p * lax.rsqrt(jnp.mean(p * p, axis=-1, keepdims=True) + RMS_EPS))
        out_ref[0] = (jnp.concatenate(parts, axis=1) * gssd_ref[...]).astype(out_ref.dtype)
    else:
        out_ref[0] = y

    @pl.when(c == pl.num_programs(1) - 1)
    def _():
        hout_ref[0] = h_sc[...]


def _ssd_scan(xbc, dtr, h0, dt_bias_l, a_lane, consts, *, direction, n_chunks, chunk0, z=None, y_fwd=None,
              d_skip_x=None, g_ssd=None):
    finish = y_fwd is not None
    q = SSD_CHUNK
    if direction == 0:
        row = lambda b, c: (b, chunk0 + c, 0)
        loc = lambda b, c: (b, c, 0)
    else:
        row = lambda b, c: (b, chunk0 + n_chunks - 1 - c, 0)
        loc = lambda b, c: (b, n_chunks - 1 - c, 0)
    const = lambda shape: pl.BlockSpec(shape, lambda b, c: tuple(0 for _ in shape))
    state_spec = pl.BlockSpec((1, SSD_GROUPS, SSD_STATE, GROUP_W), lambda b, c: (b, 0, 0, 0))
    tri, e_x, e_col = consts
    in_specs = [pl.BlockSpec((1, q, SSD_CONV_DIM), row), pl.BlockSpec((1, q, DT_PAD), row)]
    args = [xbc, dtr]
    if finish:
        in_specs += [pl.BlockSpec((1, q, SSD_INNER), row), pl.BlockSpec((1, q, SSD_INNER), loc)]
        args += [z, y_fwd]
    in_specs += [state_spec, const((1, DT_PAD)), const((1, DT_PAD)), const(tri.shape), const(e_x.shape),
                 const(e_col.shape)]
    args += [h0, dt_bias_l, a_lane, tri, e_x, e_col]
    if finish:
        in_specs += [const((1, SSD_INNER)), const((1, SSD_INNER))]
        args += [d_skip_x, g_ssd]
    n_rows = n_chunks * q
    return pl.pallas_call(
        functools.partial(_ssd_kernel, direction=direction, finish=finish),
        grid=(BATCH, n_chunks),
        in_specs=in_specs,
        out_specs=[pl.BlockSpec((1, q, SSD_INNER), loc), state_spec],
        out_shape=[jax.ShapeDtypeStruct((BATCH, n_rows, SSD_INNER), BF16 if finish else F32),
                   jax.ShapeDtypeStruct((BATCH, SSD_GROUPS, SSD_STATE, GROUP_W), F32)],
        scratch_shapes=[pltpu.VMEM((SSD_GROUPS, SSD_STATE, GROUP_W), F32)],
        compiler_params=_params(("parallel", "arbitrary")),
    )(*args)


def _dft_cs(n):
    k = np.arange(n)
    ang = 2.0 * np.pi * ((k[:, None] * k[None, :]) % n) / n
    return np.cos(ang), np.sin(ang)


FN1 = SEQ // 128
FN2 = 128


def _fnet_stage1_kernel(u_ref, m_ref, o_ref):
    o_ref[0] = jnp.dot(m_ref[...], u_ref[0].astype(BF16), preferred_element_type=F32)


def _fnet_stage1(u):
    c, s = _dft_cs(FN1)
    m = jnp.asarray(np.concatenate([c, -s], axis=0), BF16)
    cols = FN2 * FOURIER_WIDTH
    tc = 8192
    u2 = u.reshape(BATCH, TOK // FN2, cols)
    return pl.pallas_call(
        _fnet_stage1_kernel,
        grid=(BATCH, cols // tc),
        in_specs=[pl.BlockSpec((1, FN1, tc), lambda b, j: (b, 0, j)),
                  pl.BlockSpec((2 * FN1, FN1), lambda b, j: (0, 0))],
        out_specs=pl.BlockSpec((1, 2 * FN1, tc), lambda b, j: (b, 0, j)),
        out_shape=jax.ShapeDtypeStruct((BATCH, 2 * FN1, cols), F32),
        compiler_params=_params(("parallel", "arbitrary")),
    )(u2, m)


def _fnet_finish_kernel(*refs, twiddle):
    if twiddle:
        yr_ref, yi_ref, tc_ref, ts_ref, m_ref, ch_ref, o_ref = refs
        yr, yi = yr_ref[0, 0], yi_ref[0, 0]
        tc = jnp.concatenate([tc_ref[0]] * FOURIER_GROUPS, axis=1)
        ts = jnp.concatenate([ts_ref[0]] * FOURIER_GROUPS, axis=1)
        rhs = jnp.concatenate([(yr * tc + yi * ts).astype(BF16), (yi * tc - yr * ts).astype(BF16)], axis=0)
    else:
        u_ref, m_ref, ch_ref, o_ref = refs
        rhs = u_ref[0].astype(BF16)
    v = jnp.dot(m_ref[...], rhs, preferred_element_type=F32)
    r = v.shape[0] // 2
    vr, vi = v[:r].astype(BF16), v[r:].astype(BF16)
    outs = []
    for g in range(FOURIER_GROUPS):
        sl = slice(g * FOURIER_GROUP_DIM, (g + 1) * FOURIER_GROUP_DIM)
        lhs = jnp.concatenate([vr[:, sl], vi[:, sl]], axis=1)
        outs.append(jnp.dot(lhs, ch_ref[...], preferred_element_type=F32))
    o_ref[0] = jnp.concatenate(outs, axis=1).astype(o_ref.dtype)


def _channel_dft(n_pos):
    c, s = _dft_cs(FOURIER_GROUP_DIM)
    return jnp.asarray(np.concatenate([c, s], axis=0) / math.sqrt(n_pos * FOURIER_GROUP_DIM), BF16)


def _fnet_latent(y1):
    c, s = _dft_cs(FN2)
    m = jnp.asarray(np.block([[c, s], [-s, c]]), BF16)
    k1 = np.arange(FN1)[:, None, None]
    n2 = np.arange(FN2)[None, :, None]
    ang = 2.0 * np.pi * (k1 * n2) / SEQ + np.zeros((1, 1, 128))
    tw_c, tw_s = jnp.asarray(np.cos(ang), F32), jnp.asarray(np.sin(ang), F32)
    y4 = y1.reshape(BATCH, 2 * FN1, FN2, FOURIER_WIDTH)
    out = pl.pallas_call(
        functools.partial(_fnet_finish_kernel, twiddle=True),
        grid=(BATCH, FN1),
        in_specs=[pl.BlockSpec((1, 1, FN2, FOURIER_WIDTH), lambda b, k: (b, k, 0, 0)),
                  pl.BlockSpec((1, 1, FN2, FOURIER_WIDTH), lambda b, k: (b, FN1 + k, 0, 0)),
                  pl.BlockSpec((1, FN2, 128), lambda b, k: (k, 0, 0)),
                  pl.BlockSpec((1, FN2, 128), lambda b, k: (k, 0, 0)),
                  pl.BlockSpec((2 * FN2, 2 * FN2), lambda b, k: (0, 0)),
                  pl.BlockSpec((2 * FOURIER_GROUP_DIM, FOURIER_GROUP_DIM), lambda b, k: (0, 0))],
        out_specs=pl.BlockSpec((1, FN2, FOURIER_WIDTH), lambda b, k: (b, 0, k)),
        out_shape=jax.ShapeDtypeStruct((BATCH, TOK // FN1, FN1 * FOURIER_WIDTH), BF16),
        compiler_params=_params(("parallel", "arbitrary")),
    )(y4, y4, tw_c, tw_s, m, _channel_dft(SEQ))
    return out.reshape(BATCH, TOK, FOURIER_WIDTH)


def _fnet_context_kernel(u_ref, m_ref, ch_ref, f_in_ref, o_ref):
    del f_in_ref
    _fnet_finish_kernel(u_ref, m_ref, ch_ref, o_ref, twiddle=False)


def _fnet_context(u, f_all):
    c, s = _dft_cs(CTX_LEN)
    m = jnp.asarray(np.concatenate([c, -s], axis=0), BF16)
    blk = SEQ // CTX_LEN
    return pl.pallas_call(
        _fnet_context_kernel,
        grid=(BATCH,),
        in_specs=[pl.BlockSpec((1, CTX_LEN, FOURIER_WIDTH), lambda b: (b, blk, 0)),
                  pl.BlockSpec((2 * CTX_LEN, CTX_LEN), lambda b: (0, 0)),
                  pl.BlockSpec((2 * FOURIER_GROUP_DIM, FOURIER_GROUP_DIM), lambda b: (0, 0)),
                  pl.BlockSpec(memory_space=pl.ANY)],
        out_specs=pl.BlockSpec((1, CTX_LEN, FOURIER_WIDTH), lambda b: (b, blk, 0)),
        out_shape=jax.ShapeDtypeStruct(f_all.shape, f_all.dtype),
        input_output_aliases={3: 0},
        compiler_params=_params(("arbitrary",)),
    )(u, m, _channel_dft(CTX_LEN), f_all)


def _outproj_kernel(*refs, n_in):
    a_refs, w_refs = refs[:n_in], refs[n_in:2 * n_in]
    x_ref, ga_ref, o_ref = refs[2 * n_in:]
    acc = jnp.dot(a_refs[0][0], w_refs[0][...], preferred_element_type=F32)
    for a_ref, w_ref in zip(a_refs[1:], w_refs[1:]):
        acc += jnp.dot(a_ref[0], w_ref[...], preferred_element_type=F32)
    o_ref[0] = x_ref[0] + ga_ref[0] * acc


def _out_project(acts, weights, x, ga):
    n_in = len(acts)
    in_specs = [pl.BlockSpec((1, ROW_TILE, a.shape[2]), lambda b, t: (b, t, 0)) for a in acts]
    in_specs += [pl.BlockSpec(w.shape, lambda b, t: (0, 0)) for w in weights]
    in_specs += [pl.BlockSpec((1, ROW_TILE, D_MODEL), lambda b, t: (b, t, 0)),
                 pl.BlockSpec((1, 1, D_MODEL), _mod_index)]
    return pl.pallas_call(
        functools.partial(_outproj_kernel, n_in=n_in),
        grid=(BATCH, TOK // ROW_TILE),
        in_specs=in_specs,
        out_specs=pl.BlockSpec((1, ROW_TILE, D_MODEL), lambda b, t: (b, t, 0)),
        out_shape=jax.ShapeDtypeStruct((BATCH, TOK, D_MODEL), F32),
        compiler_params=_params(("parallel", "arbitrary")),
    )(*acts, *weights, x, ga)


def _attention_bias(rpb):
    tables = []
    d = np.arange(NA_QROWS)[:, None, None, None]
    c = np.arange(GRID_W)[None, :, None, None]
    i = np.arange(NA_KROWS)[None, None, :, None]
    j = np.arange(GRID_W)[None, None, None, :]
    cs = np.clip(c - NA_WIN_COLS // 2, 0, GRID_W - NA_WIN_COLS)
    col_ok = (j >= cs) & (j < cs + NA_WIN_COLS)
    col_off = np.clip(j - c + NA_WIN_COLS - 1, 0, 2 * NA_WIN_COLS - 2)
    for first_row, slab_row in ((0, 0), (NA_QROWS, 0), (GRID_H - NA_QROWS, GRID_H - NA_KROWS)):
        r = first_row + d
        rs = np.clip(r - NA_WIN_ROWS // 2, 0, GRID_H - NA_WIN_ROWS)
        key_row = slab_row + i
        row_ok = (key_row >= rs) & (key_row < rs + NA_WIN_ROWS)
        row_off = np.clip(key_row - r + NA_WIN_ROWS - 1, 0, 2 * NA_WIN_ROWS - 2)
        ok = np.broadcast_to(row_ok & col_ok, (NA_QROWS, GRID_W, NA_KROWS, GRID_W))
        ro = np.broadcast_to(row_off, ok.shape)
        co = np.broadcast_to(col_off, ok.shape)
        vals = rpb[:, ro, co]
        tables.append(jnp.where(ok[None], vals, NEG_BIG).reshape(NA_HEADS, NA_Q, NA_K))
    return jnp.stack(tables).astype(F32)


def _attn_kernel(q_ref, k_ref, v_ref, bias_ref, o_ref):
    g = pl.program_id(2)
    n_lat = SEQ // NA_Q
    q = q_ref[0] * jnp.asarray(NA_HEAD_DIM ** -0.5, BF16)
    kc = k_ref[0, SEQ:TOK, :]
    vc = v_ref[0, SEQ:TOK, :]
    lane = lax.broadcasted_iota(jnp.int32, (NA_Q, 2 * NA_HEAD_DIM), 1)
    nt = (((1,), (1,)), ((), ()))

    def head_q(hh):
        return jnp.where((lane // NA_HEAD_DIM) == hh, q, jnp.zeros_like(q))

    @pl.when(g < n_lat)
    def _():
        slab_row = jnp.clip(g * NA_QROWS - NA_WIN_ROWS // 2, 0, GRID_H - NA_KROWS)
        start = pl.multiple_of(slab_row * GRID_W, GRID_W)
        kl = k_ref[0, pl.ds(start, NA_K), :]
        vl = v_ref[0, pl.ds(start, NA_K), :]
        kind = jnp.where(g == 0, 0, jnp.where(g == n_lat - 1, 2, 1))
        outs = []
        for hh in range(2):
            qh = head_q(hh)
            s_lat = lax.dot_general(qh, kl, nt, preferred_element_type=F32) + bias_ref[kind, hh]
            s_ctx = lax.dot_general(qh, kc, nt, preferred_element_type=F32)
            m = jnp.maximum(jnp.max(s_lat, axis=-1, keepdims=True), jnp.max(s_ctx, axis=-1, keepdims=True))
            p_lat = jnp.exp(s_lat - m)
            p_ctx = jnp.exp(s_ctx - m)
            l = jnp.sum(p_lat, axis=-1, keepdims=True) + jnp.sum(p_ctx, axis=-1, keepdims=True)
            o = jnp.dot(p_lat.astype(BF16), vl, preferred_element_type=F32)
            o += jnp.dot(p_ctx.astype(BF16), vc, preferred_element_type=F32)
            outs.append(o / l)
        o_ref[0] = jnp.where((lane // NA_HEAD_DIM) == 0, outs[0], outs[1]).astype(o_ref.dtype)

    @pl.when(g == n_lat)
    def _():
        outs = []
        for hh in range(2):
            s_ctx = lax.dot_general(head_q(hh), kc, nt, preferred_element_type=F32)
            m = jnp.max(s_ctx, axis=-1, keepdims=True)
            p_ctx = jnp.exp(s_ctx - m)
            l = jnp.sum(p_ctx, axis=-1, keepdims=True)
            outs.append(jnp.dot(p_ctx.astype(BF16), vc, preferred_element_type=F32) / l)
        o_ref[0] = jnp.where((lane // NA_HEAD_DIM) == 0, outs[0], outs[1]).astype(o_ref.dtype)


def _attention(qkv, bias):
    assert NA_Q == CTX_LEN
    n_pairs = NA_HEADS // 2
    pw = 2 * NA_HEAD_DIM
    return pl.pallas_call(
        _attn_kernel,
        grid=(n_pairs, BATCH, TOK // NA_Q),
        in_specs=[pl.BlockSpec((1, NA_Q, pw), lambda p, b, g: (b, g, p)),
                  pl.BlockSpec((1, TOK, pw), lambda p, b, g: (b, 0, n_pairs + p)),
                  pl.BlockSpec((1, TOK, pw), lambda p, b, g: (b, 0, 2 * n_pairs + p)),
                  pl.BlockSpec((3, 2, NA_Q, NA_K), lambda p, b, g: (0, p, 0, 0))],
        out_specs=pl.BlockSpec((1, NA_Q, pw), lambda p, b, g: (b, g, p)),
        out_shape=jax.ShapeDtypeStruct((BATCH, TOK, D_MODEL), BF16),
        compiler_params=_params(("parallel", "parallel", "arbitrary"), 48),
    )(qkv, qkv, qkv, bias)


def _router_kernel(x_ref, g_ref, sc_ref, sh_ref, wh_ref, wl_ref, h_ref, aff_ref):
    h = _norm_mod(x_ref[0], g_ref[...], sc_ref[0], sh_ref[0])
    h_hi = h.astype(BF16)
    h_ref[0] = h_hi
    h_lo = (h - h_hi.astype(F32)).astype(BF16)
    logits = jnp.dot(h_hi, wh_ref[...], preferred_element_type=F32)
    logits += jnp.dot(h_hi, wl_ref[...], preferred_element_type=F32)
    logits += jnp.dot(h_lo, wh_ref[...], preferred_element_type=F32)
    lane = lax.broadcasted_iota(jnp.int32, logits.shape, 1)
    logits = jnp.where(lane < N_EXPERTS, logits, NEG_BIG)
    e = jnp.exp(logits - jnp.max(logits, axis=-1, keepdims=True))
    aff = e / jnp.sum(e, axis=-1, keepdims=True)
    aff_ref[0] = aff.T[:N_EXPERTS, :]


def _router(x, g, sc, sh, w_router):
    w = jnp.pad(w_router, ((0, 0), (0, 128 - N_EXPERTS)))
    w_hi = w.astype(BF16)
    w_lo = (w - w_hi.astype(F32)).astype(BF16)
    return pl.pallas_call(
        _router_kernel,
        grid=(BATCH, TOK // ROW_TILE),
        in_specs=[pl.BlockSpec((1, ROW_TILE, D_MODEL), lambda b, t: (b, t, 0)),
                  pl.BlockSpec((1, D_MODEL), lambda b, t: (0, 0)),
                  pl.BlockSpec((1, 1, D_MODEL), _mod_index),
                  pl.BlockSpec((1, 1, D_MODEL), _mod_index),
                  pl.BlockSpec((D_MODEL, 128), lambda b, t: (0, 0)),
                  pl.BlockSpec((D_MODEL, 128), lambda b, t: (0, 0))],
        out_specs=[pl.BlockSpec((1, ROW_TILE, D_MODEL), lambda b, t: (b, t, 0)),
                   pl.BlockSpec((1, N_EXPERTS, ROW_TILE), lambda b, t: (b, 0, t))],
        out_shape=[jax.ShapeDtypeStruct((BATCH, TOK, D_MODEL), BF16),
                   jax.ShapeDtypeStruct((BATCH, N_EXPERTS, TOK), F32)],
        compiler_params=_params(("parallel", "arbitrary")),
    )(x, g.reshape(1, D_MODEL), sc, sh, w_hi, w_lo)


def _ffn_kernel(x_ref, gate_ref, w1_ref, w3_ref, w2_ref, o_ref):
    f = pl.program_id(2)
    x = x_ref[0, 0]
    a = jnp.dot(x, w1_ref[0], preferred_element_type=F32)
    b = jnp.dot(x, w3_ref[0], preferred_element_type=F32)
    part = jnp.dot((_silu(a) * b).astype(BF16), w2_ref[0], preferred_element_type=F32) * gate_ref[0, 0]

    @pl.when(f == 0)
    def _():
        o_ref[0, 0] = part

    @pl.when(f > 0)
    def _():
        o_ref[0, 0] += part


def _expert_ffn(xg, gate, w1, w3, w2):
    return pl.pallas_call(
        _ffn_kernel,
        grid=(N_EXPERTS, BATCH, EXPERT_FF // FF_TILE),
        in_specs=[pl.BlockSpec((1, 1, CAP_ALL, D_MODEL), lambda e, b, f: (b, e, 0, 0)),
                  pl.BlockSpec((1, 1, CAP_ALL, 1), lambda e, b, f: (b, e, 0, 0)),
                  pl.BlockSpec((1, D_MODEL, FF_TILE), lambda e, b, f: (e, 0, f)),
                  pl.BlockSpec((1, D_MODEL, FF_TILE), lambda e, b, f: (e, 0, f)),
                  pl.BlockSpec((1, FF_TILE, D_MODEL), lambda e, b, f: (e, f, 0))],
        out_specs=pl.BlockSpec((1, 1, CAP_ALL, D_MODEL), lambda e, b, f: (b, e, 0, 0)),
        out_shape=jax.ShapeDtypeStruct((BATCH, N_EXPERTS, CAP_ALL, D_MODEL), F32),
        compiler_params=_params(("parallel", "parallel", "arbitrary")),
    )(xg, gate, w1, w3, w2)


def _final_norm_kernel(x_ref, g_ref, o_ref):
    x = x_ref[0]
    o_ref[0] = x * lax.rsqrt(jnp.mean(x * x, axis=-1, keepdims=True) + RMS_EPS) * g_ref[...]


def _final_norm(x, g):
    return pl.pallas_call(
        _final_norm_kernel,
        grid=(BATCH, SEQ // ROW_TILE),
        in_specs=[pl.BlockSpec((1, ROW_TILE, D_MODEL), lambda b, t: (b, t, 0)),
                  pl.BlockSpec((1, D_MODEL), lambda b, t: (0, 0))],
        out_specs=pl.BlockSpec((1, ROW_TILE, D_MODEL), lambda b, t: (b, t, 0)),
        out_shape=jax.ShapeDtypeStruct((BATCH, SEQ, D_MODEL), F32),
        compiler_params=_params(("parallel", "arbitrary")),
    )(x, g.reshape(1, D_MODEL))


def _even_mixer(xs, mods, g_mix, w_in, conv_w, conv_b, a_log, dt_bias, d_skip, g_ssd, w_out):
    o_dt = SSD_INNER + SSD_CONV_DIM
    w_cat = jnp.concatenate(
        [w_in[:, SSD_INNER:o_dt], w_in[:, :SSD_INNER], w_in[:, o_dt + 2 * SSD_HEADS:],
         jnp.pad(w_in[:, o_dt:o_dt + 2 * SSD_HEADS], ((0, 0), (0, DT_PAD - 2 * SSD_HEADS)))], axis=1).astype(BF16)
    xbc_pre, z, u, dtr = _project(xs, g_mix, mods[1], mods[0], w_cat,
                                  (SSD_CONV_DIM, SSD_INNER, FOURIER_WIDTH, DT_PAD), F32)
    xbc = _conv_silu(xbc_pre, conv_w, conv_b)

    pad_lane = lambda v: jnp.pad(v.reshape(1, 2 * SSD_HEADS), ((0, 0), (0, DT_PAD - 2 * SSD_HEADS)))
    dtb = pad_lane(dt_bias)
    a_lane = pad_lane(-jnp.exp(a_log))
    d_skip_x = jnp.repeat(d_skip, SSD_HEAD_DIM).reshape(1, SSD_INNER)
    g_ssd2 = g_ssd.reshape(1, SSD_INNER)
    zeros = jnp.zeros((BATCH, SSD_GROUPS, SSD_STATE, GROUP_W), F32)
    n_lat, n_ctx = SEQ // SSD_CHUNK, CTX_LEN // SSD_CHUNK
    cf, cb = _ssd_constants(0), _ssd_constants(1)
    scan = functools.partial(_ssd_scan, xbc, dtr, dt_bias_l=dtb, a_lane=a_lane)
    fin = dict(z=z, d_skip_x=d_skip_x, g_ssd=g_ssd2)
    yf_c, hf_c = scan(zeros, consts=cf, direction=0, n_chunks=n_ctx, chunk0=n_lat)
    yg_c, hb_c = scan(zeros, consts=cb, direction=1, n_chunks=n_ctx, chunk0=n_lat, y_fwd=yf_c, **fin)
    yf_l, _ = scan(hf_c, consts=cf, direction=0, n_chunks=n_lat, chunk0=0)
    yg_l, _ = scan(hb_c, consts=cb, direction=1, n_chunks=n_lat, chunk0=0, y_fwd=yf_l, **fin)
    yg = jnp.concatenate([yg_l, yg_c], axis=1)

    f_all = _fnet_latent(_fnet_stage1(u))
    f_all = _fnet_context(u, f_all)
    w_out16 = w_out.astype(BF16)
    return _out_project([yg, f_all], [w_out16[:SSD_INNER], w_out16[SSD_INNER:]], xs, mods[2])


def _odd_mixer(xs, mods, g_mix, w_qkv, rpb, w_o):
    (qkv,) = _project(xs, g_mix, mods[1], mods[0], w_qkv.astype(BF16), (3 * D_MODEL,), BF16)
    att = _attention(qkv, _attention_bias(rpb))
    return _out_project([att], [w_o.astype(BF16)], xs, mods[2])


def _moe(xs, mods, g_ffn, w_router, w1, w3, w2):
    h2, aff = _router(xs, g_ffn, mods[4], mods[3], w_router)
    gate_l, idx_l = lax.top_k(aff[:, :, :SEQ], CAP_LAT)
    gate_c, idx_c = lax.top_k(aff[:, :, SEQ:], CAP_CTX)
    idx = jnp.concatenate([idx_l, idx_c + SEQ], axis=2)
    gate = jnp.concatenate([gate_l, gate_c], axis=2)
    flat = idx.reshape(BATCH, N_EXPERTS * CAP_ALL)
    xg = jnp.take_along_axis(h2, flat[:, :, None], axis=1).reshape(BATCH, N_EXPERTS, CAP_ALL, D_MODEL)
    y = _expert_ffn(xg, gate[..., None], w1.astype(BF16), w3.astype(BF16), w2.astype(BF16))
    y = y.reshape(BATCH, N_EXPERTS * CAP_ALL, D_MODEL)
    moe = jax.vmap(lambda yb, ib: jnp.zeros((TOK, D_MODEL), F32).at[ib].add(yb))(y, flat)
    ga = mods[5]
    return jnp.concatenate([xs[:, :SEQ] + ga[:BATCH] * moe[:, :SEQ], xs[:, SEQ:] + ga[BATCH:] * moe[:, SEQ:]], axis=1)


def kernel(x, c, ctx, c_ctx, w_mod, b_mod, g_mix, g_ffn, w_in_e, conv_w, conv_b, a_log, dt_bias, d_skip, g_ssd,
           w_out_e, w_qkv, rpb, w_o, w_router, w_e1, w_e3, w_e2, g_final):
    xs = jnp.concatenate([x, ctx], axis=1)
    cc = jnp.concatenate([c, c_ctx[None], jnp.zeros((8 - BATCH - 1, D_MODEL), F32)], axis=0)
    mod_all = _modulation(cc, w_mod, b_mod)
    for i in range(DEPTH):
        j = i // 2
        m = mod_all[i].reshape(8, N_MOD, D_MODEL)
        lat = m[:BATCH]
        cx = jnp.broadcast_to(m[BATCH:BATCH + 1], (BATCH, N_MOD, D_MODEL))
        both = jnp.concatenate([lat, cx], axis=0)
        mods = [both[:, k].reshape(2 * BATCH, 1, D_MODEL) for k in range(N_MOD)]
        if i % 2 == 0:
            xs = _even_mixer(xs, mods, g_mix[i], w_in_e[j], conv_w[j], conv_b[j], a_log[j], dt_bias[j], d_skip[j],
                             g_ssd[j], w_out_e[j])
        else:
            xs = _odd_mixer(xs, mods, g_mix[i], w_qkv[j], rpb[j], w_o[j])
        xs = _moe(xs, mods, g_ffn[i], w_router[i], w_e1[i], w_e3[i], w_e2[i])
    return _final_norm(xs, g_final)
```
